```python
import jax, jax.numpy as jnp
from jax import lax
import numpy as np

D_MODEL = 2048
BATCH = 8
SEQ = 2048
DEPTH = 4
DEC_BATCH = 32
DEC_SEQ = 32
PAST_LEN = 4096

CHUNK = 64
D_POOL = 1024
POOL_WINDOWS = (2, 4, 8, 16)
N_POOL_GROUPS = 4
POOL_GROUP = D_POOL // N_POOL_GROUPS
POOL_HIST = max(POOL_WINDOWS) - 1
D_SSD = 2048
SSD_HEAD_DIM = 64
SSD_HEADS = D_SSD // SSD_HEAD_DIM
SSD_GROUPS = 2
HEADS_PER_GROUP = SSD_HEADS // SSD_GROUPS
D_STATE = 128
SSD_CONV = 4
CONV_DIM = D_SSD + 2 * SSD_GROUPS * D_STATE
D_MIX = D_POOL + D_SSD
D_IN_PROJ = D_POOL + D_SSD + CONV_DIM + SSD_HEADS
D_FF = 5632
FFN_CONV = 3
EPS = 1e-6

kernel_name = "pool_ssd_hybrid_streaming_step"


def rmsnorm(x, g):
    xf = x.astype(jnp.float32)
    y = xf * lax.rsqrt(jnp.mean(xf * xf, axis=-1, keepdims=True) + EPS)
    return (y * g.astype(jnp.float32)).astype(x.dtype)


def causal_dwconv(x, hist, w, b):
    k = w.shape[0]
    s = x.shape[1]
    ext = jnp.concatenate([hist.astype(x.dtype), x], axis=1)
    out = ext[:, 0:s] * w[0]
    for i in range(1, k):
        out = out + ext[:, i:i + s] * w[i]
    return out + b, ext[:, -(k - 1):]


def pool_mixer(u, hist, pos, w_pool, pool_scale):
    s = u.shape[1]
    ext = jnp.concatenate([hist.astype(u.dtype), u], axis=1)
    cs = jnp.cumsum(ext.astype(jnp.float32), axis=1)
    cs = jnp.pad(cs, ((0, 0), (1, 0), (0, 0)))
    end = cs[:, POOL_HIST + 1:]
    means = []
    for gi, w in enumerate(POOL_WINDOWS):
        sl = slice(gi * POOL_GROUP, (gi + 1) * POOL_GROUP)
        start = cs[:, POOL_HIST + 1 - w:POOL_HIST + 1 - w + s, sl]
        cnt = jnp.minimum(pos + 1, w).astype(jnp.float32)[None, :, None]
        means.append((end[..., sl] - start) / cnt)
    mean = jnp.concatenate(means, axis=-1)
    d = (mean - u.astype(jnp.float32)).astype(u.dtype)
    d = d.reshape(u.shape[0], s, N_POOL_GROUPS, POOL_GROUP)
    y = jnp.einsum('bsgc,gcd->bsgd', d, w_pool).reshape(u.shape)
    return y * pool_scale, ext[:, -POOL_HIST:]


def ssd_scan(xh, dt, A, Bm, Cm, D, h0):
    b, s = xh.shape[0], xh.shape[1]
    q = min(CHUNK, s)
    c = s // q
    G, J, P, N = SSD_GROUPS, HEADS_PER_GROUP, SSD_HEAD_DIM, D_STATE
    xc = (xh * dt[..., None]).reshape(b, c, q, G, J, P)
    Bc = Bm.reshape(b, c, q, G, N)
    Cc = Cm.reshape(b, c, q, G, N)
    a = jnp.moveaxis((dt * A).reshape(b, c, q, G, J), 2, -1)
    acum = jnp.cumsum(a, axis=-1)
    diff = acum[..., :, None] - acum[..., None, :]
    causal = jnp.tril(jnp.ones((q, q), dtype=bool))
    lmat = jnp.exp(jnp.where(causal, diff, -jnp.inf))
    cb = jnp.einsum('bclgn,bcsgn->bcgls', Cc, Bc)
    y_diag = jnp.einsum('bcgls,bcgjls,bcsgjp->bclgjp', cb, lmat, xc)
    decay_states = jnp.exp(acum[..., -1:] - acum)
    states = jnp.einsum('bclgn,bcgjl,bclgjp->bcgjpn', Bc, decay_states, xc)
    chunk_decay = jnp.exp(acum[..., -1])

    def step(h, inp):
        dec, st = inp
        return dec[..., None, None] * h + st, h

    h_last, h_prev = lax.scan(step, h0, (jnp.moveaxis(chunk_decay, 1, 0), jnp.moveaxis(states, 1, 0)))
    h_prev = jnp.moveaxis(h_prev, 0, 1)
    y_off = jnp.einsum('bclgn,bcgjpn,bcgjl->bclgjp', Cc, h_prev, jnp.exp(acum))
    y = (y_diag + y_off).reshape(b, s, G, J, P) + D[..., None] * xh
    return y, h_last


def ssd_mixer(z, xbc, dt_raw, conv_hist, h0, conv_w, conv_b, dt_bias, a_log, d_skip, norm_g):
    b, s = z.shape[0], z.shape[1]
    G, J, P, N = SSD_GROUPS, HEADS_PER_GROUP, SSD_HEAD_DIM, D_STATE
    xbc, new_conv = causal_dwconv(xbc, conv_hist, conv_w, conv_b)
    xbc = jax.nn.silu(xbc).astype(jnp.float32)
    xs = xbc[..., :D_SSD].reshape(b, s, G, J, P)
    Bm = xbc[..., D_SSD:D_SSD + G * N].reshape(b, s, G, N)
    Cm = xbc[..., D_SSD + G * N:].reshape(b, s, G, N)
    dt = jax.nn.softplus(dt_raw.astype(jnp.float32) + dt_bias.astype(jnp.float32)).reshape(b, s, G, J)
    A = -jnp.exp(a_log.astype(jnp.float32)).reshape(G, J)
    h0f = h0.astype(jnp.float32).reshape(b, G, J, P, N)
    y, h_last = ssd_scan(xs, dt, A, Bm, Cm, d_skip.astype(jnp.float32).reshape(G, J), h0f)
    y = y * jax.nn.silu(z.astype(jnp.float32)).reshape(b, s, G, J, P)
    yg = y.reshape(b, s, G, J * P)
    yg = yg * lax.rsqrt(jnp.mean(yg * yg, axis=-1, keepdims=True) + EPS)
    yg = yg * norm_g.astype(jnp.float32).reshape(G, J * P)
    y_out = yg.reshape(b, s, D_SSD).astype(z.dtype)
    return y_out, new_conv, h_last.reshape(b, SSD_HEADS, P, N).astype(h0.dtype)


def run_layer(x, pos, hist_pool, hist_sconv, h0, hist_fconv, p):
    xn = rmsnorm(x, p['norm_mix_pre'])
    proj = xn @ p['w_in']
    o1 = D_POOL
    o2 = o1 + D_SSD
    o3 = o2 + CONV_DIM
    u, z, xbc, dt_raw = proj[..., :o1], proj[..., o1:o2], proj[..., o2:o3], proj[..., o3:]
    y_pool, new_pool = pool_mixer(u, hist_pool, pos, p['w_pool'], p['pool_scale'])
    y_ssd, new_sconv, new_h = ssd_mixer(z, xbc, dt_raw, hist_sconv, h0, p['ssd_conv_w'], p['ssd_conv_b'],
                                        p['ssd_dt_bias'], p['ssd_a_log'], p['ssd_d'], p['ssd_norm'])
    mix = jnp.concatenate([y_pool, y_ssd], axis=-1) @ p['w_out']
    x = x + rmsnorm(mix, p['norm_mix_post'])
    hn = rmsnorm(x, p['norm_ffn_pre'])
    up = hn @ p['w_up']
    up, new_fconv = causal_dwconv(up, hist_fconv, p['ffn_conv_w'], p['ffn_conv_b'])
    gate, val = up[..., :D_FF], up[..., D_FF:]
    f = (jax.nn.silu(gate) * val) @ p['w_down']
    x = x + rmsnorm(f, p['norm_ffn_post'])
    return x, new_pool, new_sconv, new_h, new_fconv


def setup_inputs(seed: int = 0) -> dict:
    key = jax.random.key(seed)
    ks = jax.random.split(key, 32)
    f32 = jnp.float32
    nrm = lambda k, shape, sc: jax.random.normal(k, shape, f32) * sc
    gain = lambda k, shape: 1.0 + 0.1 * jax.random.normal(k, shape, f32)
    dt0 = jnp.exp(jax.random.uniform(ks[10], (DEPTH, SSD_HEADS), f32, np.log(1e-3), np.log(1e-1)))
    return {
        "x_prompt": nrm(ks[0], (BATCH, SEQ, D_MODEL), 1.0),
        "x_sample": nrm(ks[1], (DEC_BATCH, DEC_SEQ, D_MODEL), 1.0),
        "cache_pool": nrm(ks[2], (DEPTH, DEC_BATCH, POOL_HIST, D_POOL), 1.0),
        "state_ssd_conv": nrm(ks[3], (DEPTH, DEC_BATCH, SSD_CONV - 1, CONV_DIM), 1.0),
        "state_ssd": nrm(ks[4], (DEPTH, DEC_BATCH, SSD_HEADS, SSD_HEAD_DIM, D_STATE), 0.1),
        "state_ffn_conv": nrm(ks[5], (DEPTH, DEC_BATCH, FFN_CONV - 1, 2 * D_FF), 1.0),
        "norm_mix_pre": gain(ks[6], (DEPTH, D_MODEL)),
        "w_in": nrm(ks[7], (DEPTH, D_MODEL, D_IN_PROJ), D_MODEL ** -0.5),
        "w_pool": nrm(ks[8], (DEPTH, N_POOL_GROUPS, POOL_GROUP, POOL_GROUP), POOL_GROUP ** -0.5),
        "pool_scale": gain(ks[9], (DEPTH, D_POOL)),
        "ssd_conv_w": nrm(ks[11], (DEPTH, SSD_CONV, CONV_DIM), SSD_CONV ** -0.5),
        "ssd_conv_b": nrm(ks[12], (DEPTH, CONV_DIM), 0.02),
        "ssd_dt_bias": dt0 + jnp.log(-jnp.expm1(-dt0)),
        "ssd_a_log": jnp.log(jax.random.uniform(ks[13], (DEPTH, SSD_HEADS), f32, 1.0, 16.0)),
        "ssd_d": gain(ks[14], (DEPTH, SSD_HEADS)),
        "ssd_norm": gain(ks[15], (DEPTH, D_SSD)),
        "w_out": nrm(ks[16], (DEPTH, D_MIX, D_MODEL), D_MIX ** -0.5),
        "norm_mix_post": gain(ks[17], (DEPTH, D_MODEL)),
        "norm_ffn_pre": gain(ks[18], (DEPTH, D_MODEL)),
        "w_up": nrm(ks[19], (DEPTH, D_MODEL, 2 * D_FF), D_MODEL ** -0.5),
        "ffn_conv_w": nrm(ks[20], (DEPTH, FFN_CONV, 2 * D_FF), FFN_CONV ** -0.5),
        "ffn_conv_b": nrm(ks[21], (DEPTH, 2 * D_FF), 0.02),
        "w_down": nrm(ks[22], (DEPTH, D_FF, D_MODEL), D_FF ** -0.5),
        "norm_ffn_post": gain(ks[23], (DEPTH, D_MODEL)),
    }


def reference(x_prompt, x_sample, cache_pool, state_ssd_conv, state_ssd, state_ffn_conv,
              norm_mix_pre, w_in, w_pool, pool_scale, ssd_conv_w, ssd_conv_b, ssd_dt_bias,
              ssd_a_log, ssd_d, ssd_norm, w_out, norm_mix_post, norm_ffn_pre, w_up,
              ffn_conv_w, ffn_conv_b, w_down, norm_ffn_post):
    bp, sp = x_prompt.shape[0], x_prompt.shape[1]
    sd = x_sample.shape[1]
    dtp = x_prompt.dtype
    pos_p = jnp.arange(sp, dtype=jnp.int32)
    pos_s = PAST_LEN + jnp.arange(sd, dtype=jnp.int32)
    zp_pool = jnp.zeros((bp, POOL_HIST, D_POOL), dtp)
    zp_sconv = jnp.zeros((bp, SSD_CONV - 1, CONV_DIM), dtp)
    zp_h = jnp.zeros((bp, SSD_HEADS, SSD_HEAD_DIM, D_STATE), state_ssd.dtype)
    zp_fconv = jnp.zeros((bp, FFN_CONV - 1, 2 * D_FF), dtp)

    xp, xs = x_prompt, x_sample
    pool_p, pool_s, sconv_p, sconv_s, h_p, h_s, fconv_p, fconv_s = [], [], [], [], [], [], [], []
    for l in range(DEPTH):
        p = dict(norm_mix_pre=norm_mix_pre[l], w_in=w_in[l], w_pool=w_pool[l], pool_scale=pool_scale[l],
                 ssd_conv_w=ssd_conv_w[l], ssd_conv_b=ssd_conv_b[l], ssd_dt_bias=ssd_dt_bias[l],
                 ssd_a_log=ssd_a_log[l], ssd_d=ssd_d[l], ssd_norm=ssd_norm[l], w_out=w_out[l],
                 norm_mix_post=norm_mix_post[l], norm_ffn_pre=norm_ffn_pre[l], w_up=w_up[l],
                 ffn_conv_w=ffn_conv_w[l], ffn_conv_b=ffn_conv_b[l], w_down=w_down[l],
                 norm_ffn_post=norm_ffn_post[l])
        xp, a1, a2, a3, a4 = run_layer(xp, pos_p, zp_pool, zp_sconv, zp_h, zp_fconv, p)
        xs, b1, b2, b3, b4 = run_layer(xs, pos_s, cache_pool[l], state_ssd_conv[l], state_ssd[l],
                                       state_ffn_conv[l], p)
        pool_p.append(a1); sconv_p.append(a2); h_p.append(a3); fconv_p.append(a4)
        pool_s.append(b1); sconv_s.append(b2); h_s.append(b3); fconv_s.append(b4)

    return (xp, xs,
            jnp.stack(pool_p), jnp.stack(pool_s),
            jnp.stack(sconv_p), jnp.stack(sconv_s),
            jnp.stack(h_p), jnp.stack(h_s),
            jnp.stack(fconv_p), jnp.stack(fconv_s))
```

```python
import functools

import jax
import jax.numpy as jnp
import numpy as np
from jax import lax
from jax.experimental import pallas as pl
from jax.experimental.pallas import tpu as pltpu

D_MODEL = 2048
DEPTH = 4
PAST_LEN = 4096
D_POOL = 1024
POOL_WINDOWS = (2, 4, 8, 16)
POOL_GROUP = 256
POOL_HIST = 15
D_SSD = 2048
HEAD_DIM = 64
N_HEADS = 32
N_GROUPS = 2
D_STATE = 128
SSD_CONV = 4
CONV_DIM = 2560
D_MAIN = D_POOL + D_SSD + CONV_DIM
D_FF = 5632
FFN_CONV = 3
EPS = 1e-6
SSD_CHUNK = 64

LANES = 128
SUBLANES = 8
VMEM_CAP_BYTES = 60 * 1024 * 1024
HEADS_PER_GROUP = N_HEADS // N_GROUPS
PAIRS = N_HEADS // 2
POOL_PAD = 16
CONV_PAD = 8

F32 = jnp.float32
BF16 = jnp.bfloat16


def _vmem_limit(estimate_bytes):
    return int(min(VMEM_CAP_BYTES, estimate_bytes * 5 // 4 + (4 << 20)))


def _dot(a, b):
    return jnp.dot(a, b, preferred_element_type=F32)


def _rms_scale(x):
    return lax.rsqrt(jnp.mean(x * x, axis=-1, keepdims=True) + EPS)


def _silu(x):
    return x * (1.0 / (1.0 + jnp.exp(-x)))


def _in_proj_kernel(x_ref, g_ref, w_ref, wdt_ref, o_ref, dt_ref):
    x = x_ref[...]
    xn = (x * _rms_scale(x) * g_ref[...]).astype(BF16)
    o_ref[...] = _dot(xn, w_ref[...]).astype(BF16)
    dt_ref[0] = _dot(xn, wdt_ref[...])


def _in_proj(x2d, g, w_main, w_dt, tm, tn):
    t = x2d.shape[0]
    nj = D_MAIN // tn
    est = 2 * (tm * D_MODEL * 4 + D_MODEL * tn * 2 + tm * tn * 2 + D_MODEL * LANES * 2
               + tm * LANES * 4) + tm * D_MODEL * 2 + tm * tn * 4
    return pl.pallas_call(
        _in_proj_kernel,
        grid=(nj, t // tm),
        in_specs=[
            pl.BlockSpec((tm, D_MODEL), lambda j, i: (i, 0)),
            pl.BlockSpec((1, D_MODEL), lambda j, i: (0, 0)),
            pl.BlockSpec((D_MODEL, tn), lambda j, i: (0, j)),
            pl.BlockSpec((D_MODEL, LANES), lambda j, i: (0, 0)),
        ],
        out_specs=[
            pl.BlockSpec((tm, tn), lambda j, i: (i, j)),
            pl.BlockSpec((1, tm, LANES), lambda j, i: (j, i, 0)),
        ],
        out_shape=[
            jax.ShapeDtypeStruct((t, D_MAIN), BF16),
            jax.ShapeDtypeStruct((nj, t, LANES), F32),
        ],
        compiler_params=pltpu.CompilerParams(
            dimension_semantics=("arbitrary", "arbitrary"),
            vmem_limit_bytes=_vmem_limit(est)),
        name="in_proj",
    )(x2d, g, w_main, w_dt)


def _proj_norm_res_kernel(a_ref, w_ref, x_ref, g_ref, o_ref):
    m = _dot(a_ref[...], w_ref[...])
    o_ref[...] = x_ref[...] + m * _rms_scale(m) * g_ref[...]


def _proj_norm_res(a2d, w, x2d, g, tm, name):
    t, k = a2d.shape
    est = 2 * (tm * k * 2 + 2 * tm * D_MODEL * 4) + k * D_MODEL * 2 + 2 * tm * D_MODEL * 4
    return pl.pallas_call(
        _proj_norm_res_kernel,
        grid=(t // tm,),
        in_specs=[
            pl.BlockSpec((tm, k), lambda i: (i, 0)),
            pl.BlockSpec((k, D_MODEL), lambda i: (0, 0), pipeline_mode=pl.Buffered(1)),
            pl.BlockSpec((tm, D_MODEL), lambda i: (i, 0)),
            pl.BlockSpec((1, D_MODEL), lambda i: (0, 0)),
        ],
        out_specs=pl.BlockSpec((tm, D_MODEL), lambda i: (i, 0)),
        out_shape=jax.ShapeDtypeStruct((t, D_MODEL), F32),
        compiler_params=pltpu.CompilerParams(
            dimension_semantics=("arbitrary",),
            vmem_limit_bytes=_vmem_limit(est)),
        name=name,
    )(a2d, w, x2d, g)


def _ffn_up_kernel(x_ref, g_ref, wg_ref, wv_ref, cwg_ref, cwv_ref, cbg_ref, cbv_ref,
                   hg_ref, hv_ref, f_ref, ng_ref, nv_ref, ext_ref, *, nb, rows, tn):
    s = pl.program_id(2)

    @pl.when(s == 0)
    def _():
        ext_ref[:, 0:CONV_PAD, 0:tn] = hg_ref[...]
        ext_ref[:, 0:CONV_PAD, tn:2 * tn] = hv_ref[...]

    x = x_ref[...]
    hn = (x * _rms_scale(x) * g_ref[...]).astype(BF16)
    ext_ref[:, CONV_PAD:CONV_PAD + rows, 0:tn] = _dot(hn, wg_ref[...]).reshape(nb, rows, tn)
    ext_ref[:, CONV_PAD:CONV_PAD + rows, tn:2 * tn] = _dot(hn, wv_ref[...]).reshape(nb, rows, tn)

    def conv(lo, cw_ref, cb_ref):
        acc = cb_ref[...].reshape(1, 1, tn)
        for k in range(FFN_CONV):
            off = CONV_PAD - (FFN_CONV - 1) + k
            acc = acc + cw_ref[k:k + 1, :].reshape(1, 1, tn) * ext_ref[:, off:off + rows, lo:lo + tn]
        return acc

    gate = conv(0, cwg_ref, cbg_ref)
    val = conv(tn, cwv_ref, cbv_ref)
    f_ref[...] = (_silu(gate) * val).reshape(nb * rows, tn).astype(BF16)

    tail_g = ext_ref[:, rows:rows + CONV_PAD, 0:tn]
    tail_v = ext_ref[:, rows:rows + CONV_PAD, tn:2 * tn]
    ng_ref[...] = tail_g
    nv_ref[...] = tail_v
    ext_ref[:, 0:CONV_PAD, 0:tn] = tail_g
    ext_ref[:, 0:CONV_PAD, tn:2 * tn] = tail_v


def _ffn_up(x2d, g, w_up, conv_w, conv_b, hist, batch, seq, nb, rows, tn):
    nj = D_FF // tn
    n_bt = batch // nb
    n_st = seq // rows
    tm = nb * rows
    est = (2 * (tm * D_MODEL * 4 + 2 * D_MODEL * tn * 2 + tm * tn * 2 + 4 * nb * CONV_PAD * tn * 4)
           + nb * (rows + CONV_PAD) * 2 * tn * 4 + 3 * tm * tn * 4 + tm * D_MODEL * 2)
    kern = functools.partial(_ffn_up_kernel, nb=nb, rows=rows, tn=tn)
    row_blk = lambda j, b, s: (b * n_st + s, 0)
    return pl.pallas_call(
        kern,
        grid=(nj, n_bt, n_st),
        in_specs=[
            pl.BlockSpec((tm, D_MODEL), row_blk),
            pl.BlockSpec((1, D_MODEL), lambda j, b, s: (0, 0)),
            pl.BlockSpec((D_MODEL, tn), lambda j, b, s: (0, j)),
            pl.BlockSpec((D_MODEL, tn), lambda j, b, s: (0, nj + j)),
            pl.BlockSpec((FFN_CONV, tn), lambda j, b, s: (0, j)),
            pl.BlockSpec((FFN_CONV, tn), lambda j, b, s: (0, nj + j)),
            pl.BlockSpec((1, tn), lambda j, b, s: (0, j)),
            pl.BlockSpec((1, tn), lambda j, b, s: (0, nj + j)),
            pl.BlockSpec((nb, CONV_PAD, tn), lambda j, b, s: (b, 0, j)),
            pl.BlockSpec((nb, CONV_PAD, tn), lambda j, b, s: (b, 0, nj + j)),
        ],
        out_specs=[
            pl.BlockSpec((tm, tn), lambda j, b, s: (b * n_st + s, j)),
            pl.BlockSpec((nb, CONV_PAD, tn), lambda j, b, s: (b, 0, j)),
            pl.BlockSpec((nb, CONV_PAD, tn), lambda j, b, s: (b, 0, j)),
        ],
        out_shape=[
            jax.ShapeDtypeStruct((batch * seq, D_FF), BF16),
            jax.ShapeDtypeStruct((batch, CONV_PAD, D_FF), F32),
            jax.ShapeDtypeStruct((batch, CONV_PAD, D_FF), F32),
        ],
        scratch_shapes=[pltpu.VMEM((nb, rows + CONV_PAD, 2 * tn), F32)],
        compiler_params=pltpu.CompilerParams(
            dimension_semantics=("arbitrary", "arbitrary", "arbitrary"),
            vmem_limit_bytes=_vmem_limit(est)),
        name="ffn_up",
    )(x2d, g, w_up, w_up, conv_w, conv_w, conv_b, conv_b, hist, hist)


def _split3(x):
    hi = x.astype(BF16)
    r1 = x - hi.astype(F32)
    mid = r1.astype(BF16)
    lo = (r1 - mid.astype(F32)).astype(BF16)
    return hi, mid, lo


def _mixers_kernel(proj_ref, dt_ref, hpool_ref, hconv_ref, h0_ref,
                   wpool_ref, pscale_ref, convw_ref, convb_ref, dtb_ref, alog_ref, dexp_ref, ng_ref,
                   mix_ref, npool_ref, nconv_ref, nh_ref,
                   pool_ext, conv_ext, xconv_s, dt_s, ht_s, y_s, xcd_s, cd_s,
                   *, rows, q, pos0):
    s = pl.program_id(1)
    n_s = pl.num_programs(1)

    @pl.when(s == 0)
    def _():
        pool_ext[0:POOL_PAD, :] = hpool_ref[0]
        conv_ext[0:CONV_PAD, :] = hconv_ref[0]
        h0 = h0_ref[0].reshape(N_HEADS * HEAD_DIM, D_STATE)
        for g in range(N_GROUPS):
            ht_s[g] = h0[g * 1024:(g + 1) * 1024, :].T

    u = proj_ref[0, :, 0:D_POOL].astype(F32)
    pool_ext[POOL_PAD:POOL_PAD + rows, :] = u
    pos = pos0 + s * rows + lax.broadcasted_iota(jnp.int32, (rows, 1), 0)
    for gi, w in enumerate(POOL_WINDOWS):
        c0 = gi * POOL_GROUP
        acc = u[:, c0:c0 + POOL_GROUP]
        for k in range(1, w):
            acc = acc + pool_ext[POOL_PAD - k:POOL_PAD - k + rows, c0:c0 + POOL_GROUP]
        cnt = jnp.minimum(pos + 1, w).astype(F32)
        d = (acc / cnt - u[:, c0:c0 + POOL_GROUP]).astype(BF16)
        y = _dot(d, wpool_ref[gi]) * pscale_ref[:, c0:c0 + POOL_GROUP]
        mix_ref[0, :, c0:c0 + POOL_GROUP] = y.astype(BF16)
    tail_pool = pool_ext[rows:rows + POOL_PAD, :]
    npool_ref[0] = tail_pool
    pool_ext[0:POOL_PAD, :] = tail_pool

    conv_ext[CONV_PAD:CONV_PAD + rows, :] = proj_ref[0, :, D_POOL + D_SSD:D_MAIN].astype(F32)
    acc = convb_ref[...]
    for k in range(SSD_CONV):
        off = CONV_PAD - (SSD_CONV - 1) + k
        acc = acc + convw_ref[k:k + 1, :] * conv_ext[off:off + rows, :]
    xconv_s[...] = _silu(acc)
    tail_conv = conv_ext[rows:rows + CONV_PAD, :]
    nconv_ref[0] = tail_conv
    conv_ext[0:CONV_PAD, :] = tail_conv

    dtr = dt_ref[0] + dtb_ref[...]
    dt_s[...] = jnp.maximum(dtr, 0.0) + jnp.log1p(jnp.exp(-jnp.abs(dtr)))
    a_neg = -jnp.exp(alog_ref[...])

    li = lax.broadcasted_iota(jnp.int32, (q, q), 0)
    si = lax.broadcasted_iota(jnp.int32, (q, q), 1)
    tri = jnp.where(li >= si, 1.0, 0.0).astype(BF16)
    lane2q = lax.broadcasted_iota(jnp.int32, (q, 2 * q), 1)
    row2q = lax.broadcasted_iota(jnp.int32, (q, 2 * q), 0)
    first2q = lane2q < q
    causal2 = row2q >= jnp.where(first2q, lane2q, lane2q - q)
    first128 = lax.broadcasted_iota(jnp.int32, (q, LANES), 1) < HEAD_DIM

    def pick(first, arr, k):
        return jnp.where(first, arr[:, k:k + 1], arr[:, PAIRS + k:PAIRS + k + 1])

    def chunk(c, carry):
        r0 = pl.multiple_of(c * q, q)
        rsl = pl.ds(r0, q)
        dtc = dt_s[rsl, :]
        hi, mid, lo = _split3(dtc * a_neg)
        acum = _dot(tri, hi) + _dot(tri, mid) + _dot(tri, lo)
        acum_t = acum.T
        rt_all = jnp.concatenate([acum_t[0:PAIRS, :], acum_t[PAIRS:2 * PAIRS, :]], axis=1)

        for g in range(N_GROUPS):
            b_g = xconv_s[rsl, D_SSD + g * D_STATE:D_SSD + (g + 1) * D_STATE]
            c_g = xconv_s[rsl, D_SSD + (N_GROUPS + g) * D_STATE:D_SSD + (N_GROUPS + g + 1) * D_STATE]
            bt_bf = b_g.T.astype(BF16)
            c_bf = c_g.astype(BF16)
            cb = _dot(c_bf, bt_bf)
            cb2 = jnp.concatenate([cb, cb], axis=1)
            y_off = _dot(c_bf, ht_s[g].astype(BF16))
            for kk in range(PAIRS // N_GROUPS):
                k = g * (PAIRS // N_GROUPS) + kk
                lo_l = k * LANES
                ct128 = pick(first128, acum, k)
                ct_l = ct128 if 2 * q == LANES else pick(first2q, acum, k)
                lmat = jnp.where(causal2, jnp.exp(ct_l - rt_all[k:k + 1, :]), 0.0)
                m_bf = (cb2 * lmat).astype(BF16)
                xs = xconv_s[rsl, lo_l:lo_l + LANES]
                xdt = xs * pick(first128, dtc, k)
                xb = jnp.concatenate([jnp.where(first128, xdt, 0.0).astype(BF16),
                                      jnp.where(first128, 0.0, xdt).astype(BF16)], axis=0)
                ea = jnp.exp(ct128)
                y = (_dot(m_bf, xb) + ea * y_off[:, kk * LANES:(kk + 1) * LANES]
                     + dexp_ref[:, lo_l:lo_l + LANES] * xs)
                z = proj_ref[0, rsl, D_POOL + lo_l:D_POOL + lo_l + LANES].astype(F32)
                y_s[:, lo_l:lo_l + LANES] = y * _silu(z)
                last = ct128[q - 1:q, :]
                xcd_s[:, lo_l:lo_l + LANES] = (xdt * jnp.exp(last - ct128)).astype(BF16)
                cd_s[:, lo_l:lo_l + LANES] = jnp.exp(last)
            g_sl = slice(g * 1024, (g + 1) * 1024)
            yg = y_s[:, g_sl]
            yn = yg * _rms_scale(yg) * ng_ref[:, g_sl]
            mix_ref[0, rsl, D_POOL + g * 1024:D_POOL + (g + 1) * 1024] = yn.astype(BF16)
            ht_s[g] = ht_s[g] * cd_s[:, g_sl] + _dot(bt_bf, xcd_s[:, g_sl])
        return carry

    lax.fori_loop(0, rows // q, chunk, 0)

    @pl.when(s == n_s - 1)
    def _():
        for g in range(N_GROUPS):
            nh_ref[0, g * HEADS_PER_GROUP:(g + 1) * HEADS_PER_GROUP] = (
                ht_s[g].T.reshape(HEADS_PER_GROUP, HEAD_DIM, D_STATE))


def _mixers(proj, dt, hpool, hconv, h0, lw, rows, q, pos0):
    batch, seq, _ = proj.shape
    n_st = seq // rows
    kern = functools.partial(_mixers_kernel, rows=rows, q=q, pos0=pos0)
    const = lambda b, s: (0, 0)
    est = (2 * (rows * D_MAIN * 2 + rows * LANES * 4 + POOL_PAD * D_POOL * 4 + CONV_PAD * CONV_DIM * 4
                + 2 * N_HEADS * HEAD_DIM * D_STATE * 4 + 4 * POOL_GROUP * POOL_GROUP * 2
                + rows * (D_POOL + D_SSD) * 2 + POOL_PAD * D_POOL * 4 + CONV_PAD * CONV_DIM * 4)
           + (POOL_PAD + rows) * D_POOL * 4 + (CONV_PAD + 2 * rows) * CONV_DIM * 4
           + N_HEADS * HEAD_DIM * D_STATE * 4 + q * D_SSD * 6 + 3 * rows * CONV_DIM * 4)
    return pl.pallas_call(
        kern,
        grid=(batch, n_st),
        in_specs=[
            pl.BlockSpec((1, rows, D_MAIN), lambda b, s: (b, s, 0)),
            pl.BlockSpec((None, 1, rows, LANES), lambda b, s: (0, b, s, 0)),
            pl.BlockSpec((1, POOL_PAD, D_POOL), lambda b, s: (b, 0, 0)),
            pl.BlockSpec((1, CONV_PAD, CONV_DIM), lambda b, s: (b, 0, 0)),
            pl.BlockSpec((1, N_HEADS, HEAD_DIM, D_STATE), lambda b, s: (b, 0, 0, 0)),
            pl.BlockSpec((len(POOL_WINDOWS), POOL_GROUP, POOL_GROUP), lambda b, s: (0, 0, 0)),
            pl.BlockSpec((1, D_POOL), const),
            pl.BlockSpec((SSD_CONV, CONV_DIM), const),
            pl.BlockSpec((1, CONV_DIM), const),
            pl.BlockSpec((1, LANES), const),
            pl.BlockSpec((1, LANES), const),
            pl.BlockSpec((1, D_SSD), const),
            pl.BlockSpec((1, D_SSD), const),
        ],
        out_specs=[
            pl.BlockSpec((1, rows, D_POOL + D_SSD), lambda b, s: (b, s, 0)),
            pl.BlockSpec((1, POOL_PAD, D_POOL), lambda b, s: (b, 0, 0)),
            pl.BlockSpec((1, CONV_PAD, CONV_DIM), lambda b, s: (b, 0, 0)),
            pl.BlockSpec((1, N_HEADS, HEAD_DIM, D_STATE), lambda b, s: (b, 0, 0, 0)),
        ],
        out_shape=[
            jax.ShapeDtypeStruct((batch, seq, D_POOL + D_SSD), BF16),
            jax.ShapeDtypeStruct((batch, POOL_PAD, D_POOL), F32),
            jax.ShapeDtypeStruct((batch, CONV_PAD, CONV_DIM), F32),
            jax.ShapeDtypeStruct((batch, N_HEADS, HEAD_DIM, D_STATE), F32),
        ],
        scratch_shapes=[
            pltpu.VMEM((POOL_PAD + rows, D_POOL), F32),
            pltpu.VMEM((CONV_PAD + rows, CONV_DIM), F32),
            pltpu.VMEM((rows, CONV_DIM), F32),
            pltpu.VMEM((rows, LANES), F32),
            pltpu.VMEM((N_GROUPS, D_STATE, 1024), F32),
            pltpu.VMEM((q, D_SSD), F32),
            pltpu.VMEM((q, D_SSD), BF16),
            pltpu.VMEM((1, D_SSD), F32),
        ],
        compiler_params=pltpu.CompilerParams(
            dimension_semantics=("arbitrary", "arbitrary"),
            vmem_limit_bytes=_vmem_limit(est)),
        name="mixers",
    )(proj, dt, hpool, hconv, h0, lw["w_pool"], lw["pool_scale"], lw["ssd_conv_w"], lw["ssd_conv_b"],
      lw["dt_bias"], lw["a_log"], lw["d_exp"], lw["ssd_norm"])


_HEAD_PERM = np.concatenate([np.arange(0, N_HEADS, 2), np.arange(1, N_HEADS, 2)])


def _pad_lanes(v):
    return jnp.pad(v, [(0, 0)] * (v.ndim - 1) + [(0, LANES - v.shape[-1])])


def _layer_weights(l, p):
    row = lambda v: v[l].reshape(1, -1)
    w_in = p["w_in"][l]
    return dict(
        norm_mix_pre=row(p["norm_mix_pre"]),
        w_main=w_in[:, :D_MAIN].astype(BF16),
        w_dt=_pad_lanes(w_in[:, D_MAIN:][:, _HEAD_PERM]).astype(BF16),
        w_pool=p["w_pool"][l].astype(BF16),
        pool_scale=row(p["pool_scale"]),
        ssd_conv_w=p["ssd_conv_w"][l],
        ssd_conv_b=row(p["ssd_conv_b"]),
        dt_bias=_pad_lanes(p["ssd_dt_bias"][l][_HEAD_PERM].reshape(1, -1)),
        a_log=_pad_lanes(p["ssd_a_log"][l][_HEAD_PERM].reshape(1, -1)),
        d_exp=jnp.repeat(p["ssd_d"][l], HEAD_DIM).reshape(1, -1),
        ssd_norm=row(p["ssd_norm"]),
        w_out=p["w_out"][l].astype(BF16),
        norm_mix_post=row(p["norm_mix_post"]),
        norm_ffn_pre=row(p["norm_ffn_pre"]),
        w_up=p["w_up"][l].astype(BF16),
        ffn_conv_w=p["ffn_conv_w"][l],
        ffn_conv_b=row(p["ffn_conv_b"]),
        w_down=p["w_down"][l].astype(BF16),
        norm_ffn_post=row(p["norm_ffn_post"]),
    )


def _front_pad_rows(h, total):
    return jnp.pad(h, ((0, 0), (total - h.shape[1], 0), (0, 0)))


def _run_layer(x, hpool, hconv, h0, hfconv, lw, cfg):
    batch, seq, _ = x.shape
    x2d = x.reshape(batch * seq, D_MODEL)
    proj, dt = _in_proj(x2d, lw["norm_mix_pre"], lw["w_main"], lw["w_dt"], cfg["tm_in"], cfg["tn_in"])
    mix, npool, nconv, nh = _mixers(proj.reshape(batch, seq, D_MAIN), dt.reshape(-1, batch, seq, LANES),
                                    hpool, hconv, h0, lw, cfg["mix_rows"], cfg["q"], cfg["pos0"])
    x1 = _proj_norm_res(mix.reshape(batch * seq, D_POOL + D_SSD), lw["w_out"], x2d,
                        lw["norm_mix_post"], cfg["tm_out"], "out_proj")
    f, nfg, nfv = _ffn_up(x1, lw["norm_ffn_pre"], lw["w_up"], lw["ffn_conv_w"], lw["ffn_conv_b"],
                          hfconv, batch, seq, cfg["up_nb"], cfg["up_rows"], cfg["tn_up"])
    x2 = _proj_norm_res(f, lw["w_down"], x1, lw["norm_ffn_post"], cfg["tm_down"], "ffn_down")
    nfconv = jnp.concatenate([nfg[:, CONV_PAD - (FFN_CONV - 1):], nfv[:, CONV_PAD - (FFN_CONV - 1):]], axis=-1)
    return (x2.reshape(batch, seq, D_MODEL), npool[:, 1:], nconv[:, CONV_PAD - (SSD_CONV - 1):], nh, nfconv)


def _group_config(batch, seq, pos0):
    q = min(SSD_CHUNK, seq)
    if seq >= 512:
        return dict(tm_in=512, tn_in=2816, mix_rows=256, q=q, pos0=pos0, tm_out=512,
                    up_nb=1, up_rows=512, tn_up=1408, tm_down=256)
    return dict(tm_in=512, tn_in=2816, mix_rows=seq, q=q, pos0=pos0, tm_out=512,
                up_nb=batch, up_rows=seq, tn_up=512, tm_down=256)


def kernel(x_prompt, x_sample, cache_pool, state_ssd_conv, state_ssd, state_ffn_conv, norm_mix_pre, w_in, w_pool, pool_scale, ssd_conv_w, ssd_conv_b, ssd_dt_bias, ssd_a_log, ssd_d, ssd_norm, w_out, norm_mix_post, norm_ffn_pre, w_up, ffn_conv_w, ffn_conv_b, w_down, norm_ffn_post):
    params = dict(norm_mix_pre=norm_mix_pre, w_in=w_in, w_pool=w_pool, pool_scale=pool_scale,
                  ssd_conv_w=ssd_conv_w, ssd_conv_b=ssd_conv_b, ssd_dt_bias=ssd_dt_bias,
                  ssd_a_log=ssd_a_log, ssd_d=ssd_d, ssd_norm=ssd_norm, w_out=w_out,
                  norm_mix_post=norm_mix_post, norm_ffn_pre=norm_ffn_pre, w_up=w_up,
                  ffn_conv_w=ffn_conv_w, ffn_conv_b=ffn_conv_b, w_down=w_down,
                  norm_ffn_post=norm_ffn_post)
    bp, sp = x_prompt.shape[0], x_prompt.shape[1]
    bs, sd = x_sample.shape[0], x_sample.shape[1]
    cfg_p = _group_config(bp, sp, 0)
    cfg_s = _group_config(bs, sd, PAST_LEN)
    zp_pool = jnp.zeros((bp, POOL_PAD, D_POOL), F32)
    zp_conv = jnp.zeros((bp, CONV_PAD, CONV_DIM), F32)
    zp_h = jnp.zeros((bp, N_HEADS, HEAD_DIM, D_STATE), state_ssd.dtype)
    zp_fconv = jnp.zeros((bp, CONV_PAD, 2 * D_FF), F32)

    xp, xs = x_prompt, x_sample
    outs_p, outs_s = [], []
    for l in range(DEPTH):
        lw = _layer_weights(l, params)
        xp, *st_p = _run_layer(xp, zp_pool, zp_conv, zp_h, zp_fconv, lw, cfg_p)
        xs, *st_s = _run_layer(xs, _front_pad_rows(cache_pool[l], POOL_PAD),
                               _front_pad_rows(state_ssd_conv[l], CONV_PAD), state_ssd[l],
                               _front_pad_rows(state_ffn_conv[l], CONV_PAD), lw, cfg_s)
        outs_p.append(st_p)
        outs_s.append(st_s)

    stack = lambda outs, i: jnp.stack([o[i] for o in outs])
    return (xp, xs,
            stack(outs_p, 0), stack(outs_s, 0),
            stack(outs_p, 1), stack(outs_s, 1),
            stack(outs_p, 2), stack(outs_s, 2),
            stack(outs_p, 3), stack(outs_s, 3))
```
